```python
import math
import functools
import jax
import jax.numpy as jnp
from jax import lax
import numpy as np

D_MODEL = 1024
BATCH = 8
SEQ = 4096
DEPTH = 2
DEC_BATCH = 128
DEC_SEQ = 4
PAST_LEN = 16384
PAGE_SIZE = 128

N_EVEN = (DEPTH + 1) // 2
N_ODD = DEPTH // 2
D_FF = ((8 * D_MODEL // 3 + 255) // 256) * 256
MOBA_HEADS = 8
MOBA_KV_HEADS = 2
MOBA_HEAD_DIM = D_MODEL // 16
MOBA_BLOCK = 256
MOBA_TOPK = 3
DIFF_HEADS = 4
DIFF_KV_HEADS = 2
DIFF_HEAD_DIM = D_MODEL // 16
ROPE_THETA = 500000.0
MLA_HEADS = 16
MLA_Q_LORA = D_MODEL // 4
MLA_KV_LORA = D_MODEL // 4
MLA_NOPE = D_MODEL // 16
MLA_ROPE = D_MODEL // 32
MLA_V = D_MODEL // 16
MLA_ROPE_THETA = 10000.0
MLA_SCALE = (MLA_NOPE + MLA_ROPE) ** -0.5
Q_BLOCK = 128
N_MOD = 9
EPS = 1e-6
NEG = -1e30
EVEN_SPLITS = [MOBA_HEADS * MOBA_HEAD_DIM, MOBA_KV_HEADS * MOBA_HEAD_DIM, MOBA_KV_HEADS * MOBA_HEAD_DIM,
               DIFF_HEADS * 2 * DIFF_HEAD_DIM, DIFF_KV_HEADS * 2 * DIFF_HEAD_DIM, DIFF_KV_HEADS * 2 * DIFF_HEAD_DIM]
EVEN_CUTS = [int(v) for v in np.cumsum(EVEN_SPLITS)[:-1]]
EVEN_IN = sum(EVEN_SPLITS)
EVEN_OUT = MOBA_HEADS * MOBA_HEAD_DIM + DIFF_HEADS * 2 * DIFF_HEAD_DIM
ODD_IN = MLA_Q_LORA + MLA_KV_LORA + MLA_ROPE
ODD_OUT = MLA_HEADS * MLA_V

kernel_name = 'hybrid_moba_diff_mla_macaron_adaln_step'


def rmsnorm(x, g):
    xf = x.astype(jnp.float32)
    y = xf * lax.rsqrt(jnp.mean(xf * xf, axis=-1, keepdims=True) + EPS)
    return (y * g.astype(jnp.float32)).astype(x.dtype)


def rope(x, pos, theta, rot_dim):
    half = rot_dim // 2
    inv_freq = theta ** (-jnp.arange(half, dtype=jnp.float32) / half)
    ang = pos.astype(jnp.float32)[:, None] * inv_freq[None, :]
    bshape = (pos.shape[0],) + (1,) * (x.ndim - 3) + (half,)
    cos = jnp.cos(ang).reshape(bshape)
    sin = jnp.sin(ang).reshape(bshape)
    xf = x.astype(jnp.float32)
    x1, x2, rest = xf[..., :half], xf[..., half:rot_dim], xf[..., rot_dim:]
    return jnp.concatenate([x1 * cos - x2 * sin, x2 * cos + x1 * sin, rest], axis=-1).astype(x.dtype)


def adaln(c, w, b):
    mod = jax.nn.silu(c) @ w + b
    return jnp.split(mod[:, None, :], N_MOD, axis=-1)


def modulate(x, g, shift, scale):
    return rmsnorm(x, g) * (1.0 + scale) + shift


def swiglu(h, w_gu, w_down):
    a, u = jnp.split(h @ w_gu, 2, axis=-1)
    return (jax.nn.silu(a) * u) @ w_down


def causal_mask(L, q_pos):
    return jnp.arange(L)[None, :] <= q_pos[:, None]


def moba_attend(q, k, v, q_pos):
    L = k.shape[0]
    Tq = q.shape[0]
    nb = -(-L // MOBA_BLOCK)
    pad = nb * MOBA_BLOCK - L

    def blocks_of(a):
        a = jnp.pad(a, ((0, pad), (0, 0), (0, 0)))
        return a.reshape(nb, MOBA_BLOCK, MOBA_KV_HEADS, MOBA_HEAD_DIM).transpose(2, 0, 1, 3)

    kb, vb = blocks_of(k), blocks_of(v)
    kvh = jnp.arange(MOBA_HEADS) // (MOBA_HEADS // MOBA_KV_HEADS)
    k_mean = jnp.mean(kb.astype(jnp.float32), axis=2)
    gate = jnp.einsum('thd,hnd->thn', q.astype(jnp.float32), k_mean[kvh])
    q_blk = q_pos // MOBA_BLOCK
    fully_past = jnp.arange(nb)[None, None, :] < q_blk[:, None, None]
    gate = jnp.where(fully_past, gate, NEG)
    _, top = lax.top_k(gate, min(MOBA_TOPK, nb))
    own = jnp.broadcast_to(q_blk[:, None, None], top.shape[:2] + (1,))
    sel = jnp.concatenate([top, own], axis=-1)
    ok = jnp.concatenate([top < q_blk[:, None, None], jnp.ones(own.shape, dtype=bool)], axis=-1)
    kg = kb[kvh[None, :, None], sel]
    vg = vb[kvh[None, :, None], sel]
    kpos = sel[..., None] * MOBA_BLOCK + jnp.arange(MOBA_BLOCK)
    mask = ok[..., None] & (kpos <= q_pos[:, None, None, None])
    s = jnp.einsum('thd,thjkd->thjk', q, kg).astype(jnp.float32) * MOBA_HEAD_DIM ** -0.5
    s = jnp.where(mask, s, NEG)
    p = jax.nn.softmax(s.reshape(Tq, MOBA_HEADS, -1), axis=-1).reshape(s.shape)
    return jnp.einsum('thjk,thjkd->thd', p.astype(v.dtype), vg)


def diff_attend(q, k, v, q_pos, lam):
    L = k.shape[0]
    Tq = q.shape[0]
    grp = DIFF_HEADS // DIFF_KV_HEADS
    qg = q.reshape(Tq, DIFF_KV_HEADS, grp, 2, DIFF_HEAD_DIM)
    kk = k.reshape(L, DIFF_KV_HEADS, 2, DIFF_HEAD_DIM)
    s = jnp.einsum('tkgmd,lkmd->tkgml', qg, kk).astype(jnp.float32) * DIFF_HEAD_DIM ** -0.5
    s = jnp.where(causal_mask(L, q_pos)[:, None, None, None, :], s, NEG)
    p = jax.nn.softmax(s, axis=-1)
    a = p[:, :, :, 0] - lam * p[:, :, :, 1]
    o = jnp.einsum('tkgl,lkv->tkgv', a.astype(v.dtype), v)
    return o.reshape(Tq, DIFF_HEADS, 2 * DIFF_HEAD_DIM)


def mla_attend_full(q_nope, q_pe, k_nope, k_pe, v, q_pos):
    L = k_nope.shape[0]
    s = jnp.einsum('thn,lhn->thl', q_nope, k_nope) + jnp.einsum('thp,lp->thl', q_pe, k_pe)
    s = jnp.where(causal_mask(L, q_pos)[:, None, :], s.astype(jnp.float32) * MLA_SCALE, NEG)
    p = jax.nn.softmax(s, axis=-1).astype(v.dtype)
    return jnp.einsum('thl,lhv->thv', p, v)


def mla_attend_latent(q_nope, q_pe, ckv, k_pe, q_pos, w_uk, w_uv):
    L = ckv.shape[0]
    q_lat = jnp.einsum('thn,rhn->thr', q_nope, w_uk)
    s = jnp.einsum('thr,lr->thl', q_lat, ckv) + jnp.einsum('thp,lp->thl', q_pe, k_pe)
    s = jnp.where(causal_mask(L, q_pos)[:, None, :], s.astype(jnp.float32) * MLA_SCALE, NEG)
    p = jax.nn.softmax(s, axis=-1).astype(ckv.dtype)
    o_lat = jnp.einsum('thl,lr->thr', p, ckv)
    return jnp.einsum('thr,rhv->thv', o_lat, w_uv)


def prompt_sweep(attend, qs, kvs):
    S = qs[0].shape[1]
    nblk = S // Q_BLOCK
    pos_blk = jnp.arange(S, dtype=jnp.int32).reshape(nblk, Q_BLOCK)

    def per_seq(args):
        q_seq, kv_seq = args
        q_blk = tuple(a.reshape((nblk, Q_BLOCK) + a.shape[1:]) for a in q_seq)
        o = lax.map(lambda qa: attend(*qa[0], *kv_seq, qa[1]), (q_blk, pos_blk))
        return o.reshape((S,) + o.shape[2:])

    return lax.map(per_seq, (qs, kvs))


def sample_sweep(attend, qs, news, pools, layer, page_table, q_pos):
    def per_seq(args):
        q_seq, new_seq, pages = args
        kv = tuple(jnp.concatenate([pool[layer, pages].reshape((-1,) + pool.shape[3:]), new], axis=0)
                   for pool, new in zip(pools, new_seq))
        return attend(*q_seq, *kv, q_pos)

    return lax.map(per_seq, (qs, news, page_table))


def even_mixer(h, pos, w_in, g_qm, g_km, g_qd, g_kd, lam_v, g_subln, w_out, lam_init, attend):
    B, T, _ = h.shape
    qm, km, vm, qd, kd, vd = jnp.split(h @ w_in, EVEN_CUTS, axis=-1)
    qm = rope(rmsnorm(qm.reshape(B, T, MOBA_HEADS, MOBA_HEAD_DIM), g_qm), pos, ROPE_THETA, MOBA_HEAD_DIM // 4)
    km = rope(rmsnorm(km.reshape(B, T, MOBA_KV_HEADS, MOBA_HEAD_DIM), g_km), pos, ROPE_THETA, MOBA_HEAD_DIM // 4)
    vm = vm.reshape(B, T, MOBA_KV_HEADS, MOBA_HEAD_DIM)
    qd = rope(rmsnorm(qd.reshape(B, T, DIFF_HEADS, 2, DIFF_HEAD_DIM), g_qd), pos, ROPE_THETA, DIFF_HEAD_DIM // 4)
    kd = rope(rmsnorm(kd.reshape(B, T, DIFF_KV_HEADS, 2, DIFF_HEAD_DIM), g_kd), pos, ROPE_THETA, DIFF_HEAD_DIM // 4)
    kd = kd.reshape(B, T, DIFF_KV_HEADS, 2 * DIFF_HEAD_DIM)
    vd = vd.reshape(B, T, DIFF_KV_HEADS, 2 * DIFF_HEAD_DIM)
    lv = lam_v.astype(jnp.float32)
    lam = jnp.exp(jnp.sum(lv[0] * lv[1])) - jnp.exp(jnp.sum(lv[2] * lv[3])) + lam_init
    om, od = attend(qm, km, vm, qd, kd, vd, lam)
    od = rmsnorm(od, g_subln) * (1.0 - lam_init)
    y = jnp.concatenate([om.reshape(B, T, -1), od.reshape(B, T, -1)], axis=-1) @ w_out
    return y, (km, vm, kd, vd)


def attend_even_prompt(qm, km, vm, qd, kd, vd, lam):
    om = prompt_sweep(moba_attend, (qm,), (km, vm))
    od = prompt_sweep(functools.partial(diff_attend, lam=lam), (qd,), (kd, vd))
    return om, od


def attend_even_sample(qm, km, vm, qd, kd, vd, lam, pools, layer, page_table, q_pos):
    om = sample_sweep(moba_attend, (qm,), (km, vm), pools[:2], layer, page_table, q_pos)
    od = sample_sweep(functools.partial(diff_attend, lam=lam), (qd,), (kd, vd), pools[2:], layer, page_table, q_pos)
    return om, od


def odd_mixer(h, pos, w_in, g_cq, w_uq, g_q, g_ckv, g_kpe, w_ukv, w_out, attend):
    B, T, _ = h.shape
    cq, ckv, kpe = jnp.split(h @ w_in, [MLA_Q_LORA, MLA_Q_LORA + MLA_KV_LORA], axis=-1)
    q = (rmsnorm(cq, g_cq) @ w_uq).reshape(B, T, MLA_HEADS, MLA_NOPE + MLA_ROPE)
    q = rmsnorm(q, g_q)
    q_nope = q[..., :MLA_NOPE]
    q_pe = rope(q[..., MLA_NOPE:], pos, MLA_ROPE_THETA, MLA_ROPE)
    ckv = rmsnorm(ckv, g_ckv)
    kpe = rope(rmsnorm(kpe, g_kpe)[:, :, None, :], pos, MLA_ROPE_THETA, MLA_ROPE)[:, :, 0, :]
    o = attend(q_nope, q_pe, ckv, kpe, w_ukv)
    return o.reshape(B, T, -1) @ w_out, (ckv, kpe)


def attend_odd_prompt(q_nope, q_pe, ckv, kpe, w_ukv):
    kv = jnp.einsum('bsr,rhe->bshe', ckv, w_ukv)
    return prompt_sweep(mla_attend_full, (q_nope, q_pe), (kv[..., :MLA_NOPE], kpe, kv[..., MLA_NOPE:]))


def attend_odd_sample(q_nope, q_pe, ckv, kpe, w_ukv, pools, layer, page_table, q_pos):
    att = functools.partial(mla_attend_latent, w_uk=w_ukv[..., :MLA_NOPE], w_uv=w_ukv[..., MLA_NOPE:])
    return sample_sweep(att, (q_nope, q_pe), (ckv, kpe), pools, layer, page_table, q_pos)


def trunk_layer(x, c, mix, ada_w, ada_b, norm_g, f1_gu, f1_d, f2_gu, f2_d):
    sh1, sc1, g1, sh2, sc2, g2, sh3, sc3, g3 = adaln(c, ada_w, ada_b)
    x = x + 0.5 * g1 * swiglu(modulate(x, norm_g[0], sh1, sc1), f1_gu, f1_d)
    y, new_rows = mix(modulate(x, norm_g[1], sh2, sc2))
    x = x + g2 * y
    x = x + 0.5 * g3 * swiglu(modulate(x, norm_g[2], sh3, sc3), f2_gu, f2_d)
    return x, new_rows


def setup_inputs(seed: int = 0) -> dict:
    key = jax.random.key(seed)
    ks = iter(jax.random.split(key, 40))

    def nrm(shape, scale):
        return jax.random.normal(next(ks), shape, jnp.float32) * scale

    def gain(shape):
        return 1.0 + nrm(shape, 0.02)

    n_pages = PAST_LEN // PAGE_SIZE
    n_phys = (DEC_BATCH * n_pages * 5) // 4
    D = D_MODEL
    inp = {}
    inp['x_prompt'] = nrm((BATCH, SEQ, D), 1.0)
    inp['x_sample'] = nrm((DEC_BATCH, DEC_SEQ, D), 1.0)
    inp['cache_moba_k'] = nrm((N_EVEN, n_phys, PAGE_SIZE, MOBA_KV_HEADS, MOBA_HEAD_DIM), 1.0)
    inp['cache_moba_v'] = nrm((N_EVEN, n_phys, PAGE_SIZE, MOBA_KV_HEADS, MOBA_HEAD_DIM), 1.0)
    inp['cache_diff_k'] = nrm((N_EVEN, n_phys, PAGE_SIZE, DIFF_KV_HEADS, 2 * DIFF_HEAD_DIM), 1.0)
    inp['cache_diff_v'] = nrm((N_EVEN, n_phys, PAGE_SIZE, DIFF_KV_HEADS, 2 * DIFF_HEAD_DIM), 1.0)
    inp['cache_mla_ckv'] = nrm((N_ODD, n_phys, PAGE_SIZE, MLA_KV_LORA), 1.0)
    inp['cache_mla_kpe'] = nrm((N_ODD, n_phys, PAGE_SIZE, MLA_ROPE), 1.0)
    perm = jax.random.permutation(next(ks), n_phys)
    inp['page_table'] = perm[:DEC_BATCH * n_pages].reshape(DEC_BATCH, n_pages).astype(jnp.int32)
    inp['c_prompt'] = nrm((BATCH, D), 1.0)
    inp['c_sample'] = nrm((DEC_BATCH, D), 1.0)
    inp['ada_w'] = nrm((DEPTH, D, N_MOD * D), 0.5 * D ** -0.5)
    inp['ada_b'] = nrm((DEPTH, N_MOD * D), 0.02)
    inp['norm_g'] = gain((DEPTH, 3, D))
    inp['ffn1_w_gu'] = nrm((DEPTH, D, 2 * D_FF), D ** -0.5)
    inp['ffn1_w_down'] = nrm((DEPTH, D_FF, D), D_FF ** -0.5)
    inp['ffn2_w_gu'] = nrm((DEPTH, D, 2 * D_FF), D ** -0.5)
    inp['ffn2_w_down'] = nrm((DEPTH, D_FF, D), D_FF ** -0.5)
    inp['ev_w_in'] = nrm((N_EVEN, D, EVEN_IN), D ** -0.5)
    inp['ev_g_q_moba'] = gain((N_EVEN, MOBA_HEAD_DIM))
    inp['ev_g_k_moba'] = gain((N_EVEN, MOBA_HEAD_DIM))
    inp['ev_g_q_diff'] = gain((N_EVEN, DIFF_HEAD_DIM))
    inp['ev_g_k_diff'] = gain((N_EVEN, DIFF_HEAD_DIM))
    inp['ev_lambda'] = nrm((N_EVEN, 4, DIFF_HEAD_DIM), 0.1)
    inp['ev_g_subln'] = gain((N_EVEN, 2 * DIFF_HEAD_DIM))
    inp['ev_w_out'] = nrm((N_EVEN, EVEN_OUT, D), EVEN_OUT ** -0.5)
    inp['od_w_in'] = nrm((N_ODD, D, ODD_IN), D ** -0.5)
    inp['od_g_cq'] = gain((N_ODD, MLA_Q_LORA))
    inp['od_w_uq'] = nrm((N_ODD, MLA_Q_LORA, MLA_HEADS * (MLA_NOPE + MLA_ROPE)), MLA_Q_LORA ** -0.5)
    inp['od_g_q'] = gain((N_ODD, MLA_NOPE + MLA_ROPE))
    inp['od_g_ckv'] = gain((N_ODD, MLA_KV_LORA))
    inp['od_g_kpe'] = gain((N_ODD, MLA_ROPE))
    inp['od_w_ukv'] = nrm((N_ODD, MLA_KV_LORA, MLA_HEADS, MLA_NOPE + MLA_V), MLA_KV_LORA ** -0.5)
    inp['od_w_out'] = nrm((N_ODD, ODD_OUT, D), ODD_OUT ** -0.5)
    return inp


def reference(x_prompt, x_sample, cache_moba_k, cache_moba_v, cache_diff_k, cache_diff_v, cache_mla_ckv,
              cache_mla_kpe, page_table, c_prompt, c_sample, ada_w, ada_b, norm_g, ffn1_w_gu, ffn1_w_down,
              ffn2_w_gu, ffn2_w_down, ev_w_in, ev_g_q_moba, ev_g_k_moba, ev_g_q_diff, ev_g_k_diff, ev_lambda,
              ev_g_subln, ev_w_out, od_w_in, od_g_cq, od_w_uq, od_g_q, od_g_ckv, od_g_kpe, od_w_ukv, od_w_out):
    past_len = page_table.shape[1] * PAGE_SIZE
    pos_p = jnp.arange(x_prompt.shape[1], dtype=jnp.int32)
    pos_s = past_len + jnp.arange(x_sample.shape[1], dtype=jnp.int32)
    rows_p = [[] for _ in range(6)]
    rows_s = [[] for _ in range(6)]
    xp, xs = x_prompt, x_sample
    for l in range(DEPTH):
        lw = (ada_w[l], ada_b[l], norm_g[l], ffn1_w_gu[l], ffn1_w_down[l], ffn2_w_gu[l], ffn2_w_down[l])
        if l % 2 == 0:
            e = l // 2
            ew = dict(w_in=ev_w_in[e], g_qm=ev_g_q_moba[e], g_km=ev_g_k_moba[e], g_qd=ev_g_q_diff[e],
                      g_kd=ev_g_k_diff[e], lam_v=ev_lambda[e], g_subln=ev_g_subln[e], w_out=ev_w_out[e],
                      lam_init=0.8 - 0.6 * math.exp(-0.3 * l))
            att_s = functools.partial(attend_even_sample, pools=(cache_moba_k, cache_moba_v, cache_diff_k, cache_diff_v),
                                      layer=e, page_table=page_table, q_pos=pos_s)
            mix_p = functools.partial(even_mixer, pos=pos_p, attend=attend_even_prompt, **ew)
            mix_s = functools.partial(even_mixer, pos=pos_s, attend=att_s, **ew)
            off = 0
        else:
            o = l // 2
            ow = dict(w_in=od_w_in[o], g_cq=od_g_cq[o], w_uq=od_w_uq[o], g_q=od_g_q[o], g_ckv=od_g_ckv[o],
                      g_kpe=od_g_kpe[o], w_ukv=od_w_ukv[o], w_out=od_w_out[o])
            att_s = functools.partial(attend_odd_sample, pools=(cache_mla_ckv, cache_mla_kpe), layer=o,
                                      page_table=page_table, q_pos=pos_s)
            mix_p = functools.partial(odd_mixer, pos=pos_p, attend=attend_odd_prompt, **ow)
            mix_s = functools.partial(odd_mixer, pos=pos_s, attend=att_s, **ow)
            off = 4
        xp, new_p = trunk_layer(xp, c_prompt, mix_p, *lw)
        xs, new_s = trunk_layer(xs, c_sample, mix_s, *lw)
        for i, (rp, rs) in enumerate(zip(new_p, new_s)):
            rows_p[off + i].append(rp)
            rows_s[off + i].append(rs)
    p_moba_k, p_moba_v, p_diff_k, p_diff_v, p_mla_ckv, p_mla_kpe = [jnp.stack(r, axis=0) for r in rows_p]
    s_moba_k, s_moba_v, s_diff_k, s_diff_v, s_mla_ckv, s_mla_kpe = [jnp.stack(r, axis=0) for r in rows_s]
    return (xp, xs, p_moba_k, p_moba_v, p_diff_k, p_diff_v, p_mla_ckv, p_mla_kpe,
            s_moba_k, s_moba_v, s_diff_k, s_diff_v, s_mla_ckv, s_mla_kpe)
```

```python
import functools
import math

import jax
import jax.numpy as jnp
import numpy as np
from jax import lax
from jax.experimental import pallas as pl
from jax.experimental.pallas import tpu as pltpu

F32 = jnp.float32
BF16 = jnp.bfloat16

D_MODEL = 1024
PAGE = 128
D_FF = ((8 * D_MODEL // 3 + 255) // 256) * 256
MOBA_HEADS, MOBA_KV, MOBA_D, MOBA_BLOCK, MOBA_TOPK = 8, 2, 64, 256, 3
DIFF_HEADS, DIFF_KV, DIFF_D = 4, 2, 64
ROPE_THETA = 500000.0
MLA_HEADS, MLA_QL, MLA_KVL, MLA_NOPE, MLA_ROPE, MLA_V = 16, 256, 256, 64, 32, 64
MLA_ROPE_THETA = 10000.0
MLA_SCALE = (MLA_NOPE + MLA_ROPE) ** -0.5
N_MOD = 9
EPS = 1e-6
MASK = -(2.0 ** 100)
BELOW_MASK = -3.0e38
EVEN_IN = 1792
LANES = 128
VMEM_LIMIT = 56 * 1024 * 1024


def _cp(sem, vmem=VMEM_LIMIT):
    return pltpu.CompilerParams(dimension_semantics=sem, vmem_limit_bytes=vmem)


def _const_spec(shape):
    nd = len(shape)
    return pl.BlockSpec(shape, lambda *_: (0,) * nd, pipeline_mode=pl.Buffered(1))


def _dot(a, b):
    return jnp.dot(a, b, preferred_element_type=F32)


def _dot_nt(a, b, precision=None):
    return lax.dot_general(a, b, (((1,), (1,)), ((), ())), precision=precision,
                           preferred_element_type=F32)


def _sigmoid(x):
    return 1.0 / (1.0 + jnp.exp(-x))


def _rms(x, g):
    return x * lax.rsqrt(jnp.mean(x * x, axis=-1, keepdims=True) + EPS) * g


def _mod_rows(mod_ref, i, per_row):
    if per_row:
        return mod_ref[0], mod_ref[1], mod_ref[2]
    return mod_ref[3 * i:3 * i + 1, :], mod_ref[3 * i + 1:3 * i + 2, :], mod_ref[3 * i + 2:3 * i + 3, :]


def _mod_spec(i, per_row, tm, d):
    if per_row:
        return pl.BlockSpec((3, tm, d), lambda r, i=i: (i, r, 0))
    return None


def _seg_mean_sq(x, seg_ref, inv_n):
    w = x.shape[-1]
    sq = x * x
    hi = sq.astype(BF16)
    lo = (sq - hi.astype(F32)).astype(BF16)
    seg = seg_ref[:w, :w]
    return (_dot(hi, seg) + _dot(lo, seg)) * inv_n


def _rope(y, c, s1, s2, half):
    w = y.shape[-1]
    reps = w // LANES
    if reps > 1:
        c = jnp.concatenate([c] * reps, axis=1)
        s1 = jnp.concatenate([s1] * reps, axis=1)
        s2 = jnp.concatenate([s2] * reps, axis=1)
    return y * c + pltpu.roll(y, w - half, 1) * s1 + pltpu.roll(y, half, 1) * s2


def _adaln_body(c_ref, w_ref, b_ref, o_ref):
    c = c_ref[...]
    s = (c * _sigmoid(c)).astype(BF16)
    o_ref[...] = _dot(s, w_ref[...].astype(BF16)) + b_ref[...]


def _adaln(c_all, ada_w, ada_b):
    depth, d, n9 = ada_w.shape
    nc = c_all.shape[0]
    tn = 1536
    return pl.pallas_call(
        _adaln_body,
        out_shape=jax.ShapeDtypeStruct((depth, nc, n9), F32),
        grid=(depth, n9 // tn),
        in_specs=[pl.BlockSpec((nc, d), lambda l, j: (0, 0)),
                  pl.BlockSpec((None, d, tn), lambda l, j: (l, 0, j)),
                  pl.BlockSpec((None, 1, tn), lambda l, j: (l, 0, j))],
        out_specs=pl.BlockSpec((None, nc, tn), lambda l, j: (l, 0, j)),
        compiler_params=_cp(("arbitrary", "arbitrary")),
        name="adaln",
    )(c_all, ada_w, ada_b.reshape(depth, 1, n9))


FF_CHUNK = 256


def _ffn_body(x_ref, mod_ref, g_ref, wgu_ref, wd_ref, o_ref, act_ref, *, mi, per_row):
    x = x_ref[...]
    sh, sc, gt = _mod_rows(mod_ref, mi, per_row)
    h = (_rms(x, g_ref[...]) * (1.0 + sc) + sh).astype(BF16)
    ff = wd_ref.shape[0]
    for c0 in range(0, ff, FF_CHUNK):
        a = _dot(h, wgu_ref[:, c0:c0 + FF_CHUNK])
        u = _dot(h, wgu_ref[:, ff + c0:ff + c0 + FF_CHUNK])
        act_ref[:, c0:c0 + FF_CHUNK] = (a * _sigmoid(a) * u).astype(BF16)
    y = _dot(act_ref[...], wd_ref[...])
    o_ref[...] = x + (0.5 * gt) * y


def _ffn_prompt(x, mod, g, wgu, wd, mi, tm=512):
    b, s, d = x.shape
    ff = wd.shape[0]
    return pl.pallas_call(
        functools.partial(_ffn_body, mi=mi, per_row=False),
        out_shape=jax.ShapeDtypeStruct(x.shape, F32),
        grid=(b, s // tm),
        in_specs=[pl.BlockSpec((None, tm, d), lambda i, j: (i, j, 0)),
                  pl.BlockSpec((None, N_MOD, d), lambda i, j: (i, 0, 0)),
                  _const_spec((1, d)), _const_spec((d, 2 * ff)), _const_spec((ff, d))],
        out_specs=pl.BlockSpec((None, tm, d), lambda i, j: (i, j, 0)),
        scratch_shapes=[pltpu.VMEM((tm, ff), BF16)],
        compiler_params=_cp(("arbitrary", "arbitrary")),
        name="ffn_prompt",
    )(x, mod, g, wgu, wd)


def _ffn_sample(x, mod, g, wgu, wd, mi, tm=256):
    nt, d = x.shape
    ff = wd.shape[0]
    return pl.pallas_call(
        functools.partial(_ffn_body, mi=mi, per_row=True),
        out_shape=jax.ShapeDtypeStruct(x.shape, F32),
        grid=(nt // tm,),
        in_specs=[pl.BlockSpec((tm, d), lambda r: (r, 0)),
                  _mod_spec(mi, True, tm, d),
                  _const_spec((1, d)), _const_spec((d, 2 * ff)), _const_spec((ff, d))],
        out_specs=pl.BlockSpec((tm, d), lambda r: (r, 0)),
        scratch_shapes=[pltpu.VMEM((tm, ff), BF16)],
        compiler_params=_cp(("arbitrary",)),
        name="ffn_sample",
    )(x, mod, g, wgu, wd)


def _even_proj_body(x_ref, mod_ref, g_ref, win_ref, gain_ref, c_ref, s1_ref, s2_ref, seg_ref,
                    qm_ref, km_ref, vm_ref, qd_ref, kd_ref, vd_ref, *, per_row):
    x = x_ref[...]
    sh, sc, _ = _mod_rows(mod_ref, 1, per_row)
    h = (_rms(x, g_ref[...]) * (1.0 + sc) + sh).astype(BF16)
    p = _dot(h, win_ref[...])
    c, s1, s2 = c_ref[...], s1_ref[...], s2_ref[...]

    def norm_rope(lo, width, out_ref):
        for g0 in range(0, width, 256):
            w = min(256, width - g0)
            pg = p[:, lo + g0:lo + g0 + w]
            y = pg * lax.rsqrt(_seg_mean_sq(pg, seg_ref, 1.0 / MOBA_D) + EPS) * gain_ref[:, lo + g0:lo + g0 + w]
            out_ref[:, g0:g0 + w] = _rope(y, c, s1, s2, MOBA_D // 8)

    norm_rope(0, 512, qm_ref)
    norm_rope(512, 128, km_ref)
    vm_ref[...] = p[:, 640:768]
    norm_rope(768, 512, qd_ref)
    norm_rope(1280, 256, kd_ref)
    vd_ref[...] = p[:, 1536:1792]


def _even_proj(x, mod, g, win, gain, tabs, seg, per_row, tm):
    d = x.shape[-1]
    widths = (512, 128, 128, 512, 256, 256)
    if per_row:
        nt = x.shape[0]
        grid = (nt // tm,)
        xs = pl.BlockSpec((tm, d), lambda r: (r, 0))
        ms = _mod_spec(1, True, tm, d)
        ts = pl.BlockSpec((tm, LANES), lambda r: (r, 0))
        outs = [pl.BlockSpec((tm, w), lambda r: (r, 0)) for w in widths]
        shapes = [jax.ShapeDtypeStruct((nt, w), F32) for w in widths]
        sem = ("arbitrary",)
    else:
        b, s, _ = x.shape
        grid = (b, s // tm)
        xs = pl.BlockSpec((None, tm, d), lambda i, j: (i, j, 0))
        ms = pl.BlockSpec((None, N_MOD, d), lambda i, j: (i, 0, 0))
        ts = pl.BlockSpec((tm, LANES), lambda i, j: (j, 0))
        outs = [pl.BlockSpec((None, tm, w), lambda i, j: (i, j, 0)) for w in widths]
        shapes = [jax.ShapeDtypeStruct((b, s, w), F32) for w in widths]
        sem = ("arbitrary", "arbitrary")
    return pl.pallas_call(
        functools.partial(_even_proj_body, per_row=per_row),
        out_shape=shapes,
        grid=grid,
        in_specs=[xs, ms, _const_spec((1, d)), _const_spec((d, EVEN_IN)), _const_spec((1, EVEN_IN)),
                  ts, ts, ts, _const_spec((256, 256))],
        out_specs=outs,
        compiler_params=_cp(sem),
        name="even_proj",
    )(x, mod, g, win, gain, *tabs, seg)


TQ = 256


def _causal_keep(rows):
    r = lax.broadcasted_iota(jnp.int32, (rows, TQ), 0) % TQ
    c = lax.broadcasted_iota(jnp.int32, (rows, TQ), 1)
    return c <= r


def _flash(qi, chains, post_scale=None):
    def scores(q, kt):
        s = _dot_nt(q, kt)
        return s if post_scale is None else s * post_scale

    init = []
    for q, k_tile, v_tile in chains:
        s = jnp.where(_causal_keep(q.shape[0]), scores(q, k_tile(qi)), MASK)
        m = jnp.max(s, axis=1, keepdims=True)
        p = jnp.exp(s - m)
        init += [m, jnp.sum(p, axis=1, keepdims=True), _dot(p.astype(BF16), v_tile(qi))]

    def step(j, carry):
        out = []
        for ci, (q, k_tile, v_tile) in enumerate(chains):
            m, l, acc = carry[3 * ci:3 * ci + 3]
            s = scores(q, k_tile(j))
            m_new = jnp.maximum(m, jnp.max(s, axis=1, keepdims=True))
            alpha = jnp.exp(m - m_new)
            p = jnp.exp(s - m_new)
            out += [m_new, alpha * l + jnp.sum(p, axis=1, keepdims=True),
                    alpha * acc + _dot(p.astype(BF16), v_tile(j))]
        return tuple(out)

    fin = lax.fori_loop(0, qi, step, tuple(init))
    return [fin[3 * ci + 2] / fin[3 * ci + 1] for ci in range(len(chains))]


def _lane_iota(shape):
    return lax.broadcasted_iota(jnp.int32, shape, len(shape) - 1)


def _moba_prompt_body(q_ref, k_ref, v_ref, o_ref, kaug_ref, vb_ref, kmrow_ref):
    qi = pl.program_id(1)
    s_len = k_ref.shape[0]
    nblk = s_len // MOBA_BLOCK
    hpk = MOBA_HEADS // MOBA_KV

    @pl.when(qi == 0)
    def _():
        kmrow_ref[...] = jnp.zeros(kmrow_ref.shape, F32)
        vb_ref[...] = v_ref[...].astype(BF16)
        lane = _lane_iota((MOBA_BLOCK, LANES))
        for n in range(nblk):
            kblk = k_ref[n * MOBA_BLOCK:(n + 1) * MOBA_BLOCK, :]
            kmean = jnp.mean(kblk, axis=0, keepdims=True)
            onehot = jnp.where(lane - MOBA_D == n, 1.0, 0.0)
            krot = pltpu.roll(kblk, MOBA_D, 1)
            kaug_ref[n * MOBA_BLOCK:(n + 1) * MOBA_BLOCK, 0:LANES] = jnp.where(lane < MOBA_D, kblk, onehot).astype(BF16)
            kaug_ref[n * MOBA_BLOCK:(n + 1) * MOBA_BLOCK, LANES:2 * LANES] = jnp.where(lane < MOBA_D, krot, onehot).astype(BF16)
            l1 = _lane_iota((1, LANES))
            mrot = pltpu.roll(kmean, MOBA_D, 1)
            kmrow_ref[0, MOBA_D + n:MOBA_D + n + 1, :] = jnp.where(l1 < MOBA_D, kmean, 0.0)
            kmrow_ref[1, MOBA_D + n:MOBA_D + n + 1, :] = jnp.where(l1 >= MOBA_D, mrot, 0.0)
            kmrow_ref[2, MOBA_D + n:MOBA_D + n + 1, :] = jnp.where(l1 < MOBA_D, mrot, 0.0)
            kmrow_ref[3, MOBA_D + n:MOBA_D + n + 1, :] = jnp.where(l1 >= MOBA_D, kmean, 0.0)

    lane = _lane_iota((TQ, LANES))
    blk = lane - MOBA_D
    qaug = []
    for h in range(MOBA_HEADS):
        k, par = h // hpk, h % 2
        qg = q_ref[:, (h // 2) * LANES:(h // 2 + 1) * LANES]
        gate = _dot_nt(qg, kmrow_ref[2 * k + par], precision=lax.Precision.HIGHEST)
        valid = (blk >= 0) & (blk < qi)
        g = jnp.where(valid, gate, MASK)
        g = jnp.where((blk >= 0) & (blk < nblk), g, BELOW_MASK)
        picked = jnp.zeros((TQ, LANES), jnp.bool_)
        for _ in range(min(MOBA_TOPK, nblk)):
            mx = jnp.max(g, axis=1, keepdims=True)
            first = jnp.min(jnp.where(g == mx, lane, 2 * LANES), axis=1, keepdims=True)
            hit = lane == first
            picked = picked | hit
            g = jnp.where(hit, BELOW_MASK, g)
        attend = (picked & valid) | (blk == qi)
        bias = jnp.where(attend, 0.0, MASK)
        qs = qg if par == 0 else pltpu.roll(qg, MOBA_D, 1)
        qaug.append(jnp.where(lane < MOBA_D, qs * (MOBA_D ** -0.5), bias).astype(BF16))

    chains = []
    for k in range(MOBA_KV):
        qstack = jnp.concatenate(qaug[k * hpk:(k + 1) * hpk], axis=0)
        chains.append((qstack,
                       lambda j, k=k: kaug_ref[pl.ds(pl.multiple_of(j * TQ, TQ), TQ), k * LANES:(k + 1) * LANES],
                       lambda j: vb_ref[pl.ds(pl.multiple_of(j * TQ, TQ), TQ), :]))
    outs = _flash(qi, chains)
    for g2 in range(MOBA_HEADS // 2):
        k = (2 * g2) // hpk
        a = outs[k][((2 * g2) % hpk) * TQ:((2 * g2) % hpk + 1) * TQ, :]
        b = outs[k][((2 * g2 + 1) % hpk) * TQ:((2 * g2 + 1) % hpk + 1) * TQ, :]
        if k == 0:
            pair = jnp.where(lane < MOBA_D, a, pltpu.roll(b, MOBA_D, 1))
        else:
            pair = jnp.where(lane < MOBA_D, pltpu.roll(a, MOBA_D, 1), b)
        o_ref[:, g2 * LANES:(g2 + 1) * LANES] = pair.astype(BF16)


def _moba_prompt(qm, km, vm):
    b, s, _ = qm.shape
    return pl.pallas_call(
        _moba_prompt_body,
        out_shape=jax.ShapeDtypeStruct((b, s, 512), BF16),
        grid=(b, s // TQ),
        in_specs=[pl.BlockSpec((None, TQ, 512), lambda i, j: (i, j, 0)),
                  pl.BlockSpec((None, s, 128), lambda i, j: (i, 0, 0)),
                  pl.BlockSpec((None, s, 128), lambda i, j: (i, 0, 0))],
        out_specs=pl.BlockSpec((None, TQ, 512), lambda i, j: (i, j, 0)),
        scratch_shapes=[pltpu.VMEM((s, 256), BF16), pltpu.VMEM((s, 128), BF16),
                        pltpu.VMEM((4, LANES, LANES), F32)],
        compiler_params=_cp(("arbitrary", "arbitrary")),
        name="moba_prompt",
    )(qm, km, vm)


def _lambda(lv_ref, lam_init):
    lv = lv_ref[...]
    s1 = jnp.sum(lv[0:1, :] * lv[1:2, :], axis=1, keepdims=True)
    s2 = jnp.sum(lv[2:3, :] * lv[3:4, :], axis=1, keepdims=True)
    return jnp.exp(s1) - jnp.exp(s2) + lam_init


def _diff_prompt_body(q_ref, k_ref, v_ref, lv_ref, gs_ref, o_ref, kb_ref, vb_ref, *, lam_init):
    qi = pl.program_id(1)
    grp = DIFF_HEADS // DIFF_KV

    @pl.when(qi == 0)
    def _():
        kb_ref[...] = k_ref[...].astype(BF16)
        vb_ref[...] = v_ref[...].astype(BF16)

    lam = _lambda(lv_ref, lam_init)
    lane = _lane_iota((TQ, LANES))
    chains = []
    for k in range(DIFF_KV):
        rows = []
        for m in range(2):
            for g in range(grp):
                h = k * grp + g
                qg = q_ref[:, h * LANES:(h + 1) * LANES] * (DIFF_D ** -0.5)
                rows.append(jnp.where((lane // DIFF_D) == m, qg, 0.0).astype(BF16))
        chains.append((jnp.concatenate(rows, axis=0),
                       lambda j, k=k: kb_ref[pl.ds(pl.multiple_of(j * TQ, TQ), TQ), k * LANES:(k + 1) * LANES],
                       lambda j, k=k: vb_ref[pl.ds(pl.multiple_of(j * TQ, TQ), TQ), k * LANES:(k + 1) * LANES]))
    outs = _flash(qi, chains)
    for k in range(DIFF_KV):
        for g in range(grp):
            a0 = outs[k][g * TQ:(g + 1) * TQ, :]
            a1 = outs[k][(grp + g) * TQ:(grp + g + 1) * TQ, :]
            o = _rms(a0 - lam * a1, gs_ref[...]) * (1.0 - lam_init)
            h = k * grp + g
            o_ref[:, h * LANES:(h + 1) * LANES] = o.astype(BF16)


def _diff_prompt(qd, kd, vd, lam_v, g_subln, lam_init):
    b, s, _ = qd.shape
    return pl.pallas_call(
        functools.partial(_diff_prompt_body, lam_init=lam_init),
        out_shape=jax.ShapeDtypeStruct((b, s, 512), BF16),
        grid=(b, s // TQ),
        in_specs=[pl.BlockSpec((None, TQ, 512), lambda i, j: (i, j, 0)),
                  pl.BlockSpec((None, s, 256), lambda i, j: (i, 0, 0)),
                  pl.BlockSpec((None, s, 256), lambda i, j: (i, 0, 0)),
                  _const_spec((4, DIFF_D)), _const_spec((1, 2 * DIFF_D))],
        out_specs=pl.BlockSpec((None, TQ, 512), lambda i, j: (i, j, 0)),
        scratch_shapes=[pltpu.VMEM((s, 256), BF16), pltpu.VMEM((s, 256), BF16)],
        compiler_params=_cp(("arbitrary", "arbitrary")),
        name="diff_prompt",
    )(qd, kd, vd, lam_v, g_subln.reshape(1, -1))


def _out_proj_body(*refs, n_in, per_row):
    x_ref, mod_ref = refs[0], refs[1]
    o_refs = refs[2:2 + n_in]
    w_ref, out_ref = refs[2 + n_in], refs[3 + n_in]
    _, _, gt = _mod_rows(mod_ref, 1, per_row)
    y = None
    off = 0
    for r in o_refs:
        w = r.shape[-1]
        t = _dot(r[...].astype(BF16), w_ref[off:off + w, :])
        y = t if y is None else y + t
        off += w
    out_ref[...] = x_ref[...] + gt * y


def _out_proj(x, mod, parts, w, per_row, tm):
    d = x.shape[-1]
    n_in = len(parts)
    if per_row:
        nt = x.shape[0]
        grid = (nt // tm,)
        xs = pl.BlockSpec((tm, d), lambda r: (r, 0))
        ms = _mod_spec(1, True, tm, d)
        ps = [pl.BlockSpec((tm, p.shape[-1]), lambda r: (r, 0)) for p in parts]
        sem = ("arbitrary",)
    else:
        b, s, _ = x.shape
        grid = (b, s // tm)
        xs = pl.BlockSpec((None, tm, d), lambda i, j: (i, j, 0))
        ms = pl.BlockSpec((None, N_MOD, d), lambda i, j: (i, 0, 0))
        ps = [pl.BlockSpec((None, tm, p.shape[-1]), lambda i, j: (i, j, 0)) for p in parts]
        sem = ("arbitrary", "arbitrary")
    return pl.pallas_call(
        functools.partial(_out_proj_body, n_in=n_in, per_row=per_row),
        out_shape=jax.ShapeDtypeStruct(x.shape, F32),
        grid=grid,
        in_specs=[xs, ms] + ps + [_const_spec(w.shape)],
        out_specs=xs,
        compiler_params=_cp(sem),
        name="out_proj",
    )(x, mod, *parts, w)


ODD_IN_PAD = 640


def _odd_proj_body(x_ref, mod_ref, g_ref, win_ref, gcq_ref, wuq_ref, gq_ref, gckv_ref, gkpe_ref,
                   cq_ref, sq1_ref, sq2_ref, ck_ref, sk1_ref, sk2_ref, seg_ref, *rest, per_row):
    x = x_ref[...]
    sh, sc, _ = _mod_rows(mod_ref, 1, per_row)
    h = (_rms(x, g_ref[...]) * (1.0 + sc) + sh).astype(BF16)
    p = _dot(h, win_ref[...])
    cq = _rms(p[:, 0:MLA_QL], gcq_ref[...]).astype(BF16)
    ckv = _rms(p[:, MLA_QL:MLA_QL + MLA_KVL], gckv_ref[...])
    kp = p[:, 512:640]
    kp = kp * lax.rsqrt(jnp.sum(kp * kp, axis=-1, keepdims=True) * (1.0 / MLA_ROPE) + EPS) * gkpe_ref[...]
    kp = _rope(kp, ck_ref[...], sk1_ref[...], sk2_ref[...], MLA_ROPE // 2)
    q = _dot(cq, wuq_ref[...])
    cqt, s1t, s2t = cq_ref[...], sq1_ref[...], sq2_ref[...]
    if per_row:
        q_ref, ckv_ref, kpe_ref = rest
    else:
        wk_ref, wv_ref, q_ref, ckv_ref, kpe_ref, kf_ref, vf_ref = rest
    for g0 in range(0, MLA_HEADS * LANES, 256):
        qg = q[:, g0:g0 + 256]
        y = qg * lax.rsqrt(_seg_mean_sq(qg, seg_ref, 1.0 / (MLA_NOPE + MLA_ROPE)) + EPS) * gq_ref[...]
        q_ref[:, g0:g0 + 256] = _rope(y, cqt, s1t, s2t, MLA_ROPE // 2).astype(BF16)
    ckv_ref[...] = ckv
    kpe_ref[...] = kp[:, 0:MLA_ROPE]
    if not per_row:
        kvin = jnp.concatenate([ckv, kp], axis=1).astype(BF16)
        kf_ref[...] = _dot(kvin, wk_ref[...]).astype(BF16)
        vf_ref[...] = _dot(kvin[:, 0:MLA_KVL], wv_ref[...]).astype(BF16)


def _odd_proj(x, mod, g, win, gcq, wuq, gq, gckv, gkpe, qtabs, ktabs, seg, wk, wv, per_row, tm):
    d = x.shape[-1]
    qw = MLA_HEADS * LANES
    consts = [_const_spec((1, d)), _const_spec((d, ODD_IN_PAD)), _const_spec((1, MLA_QL)),
              _const_spec((MLA_QL, qw)), _const_spec((1, 256)), _const_spec((1, MLA_KVL)),
              _const_spec((1, LANES))]
    if per_row:
        nt = x.shape[0]
        grid = (nt // tm,)
        xs = pl.BlockSpec((tm, d), lambda r: (r, 0))
        ms = _mod_spec(1, True, tm, d)
        ts = pl.BlockSpec((tm, LANES), lambda r: (r, 0))
        widths = (qw, MLA_KVL, MLA_ROPE)
        dts = (BF16, F32, F32)
        outs = [pl.BlockSpec((tm, w), lambda r: (r, 0)) for w in widths]
        shapes = [jax.ShapeDtypeStruct((nt, w), dt) for w, dt in zip(widths, dts)]
        extra_specs, extra = [], []
        sem = ("arbitrary",)
    else:
        b, s, _ = x.shape
        grid = (b, s // tm)
        xs = pl.BlockSpec((None, tm, d), lambda i, j: (i, j, 0))
        ms = pl.BlockSpec((None, N_MOD, d), lambda i, j: (i, 0, 0))
        ts = pl.BlockSpec((tm, LANES), lambda i, j: (j, 0))
        widths = (qw, MLA_KVL, MLA_ROPE, qw, MLA_HEADS * MLA_V)
        dts = (BF16, F32, F32, BF16, BF16)
        outs = [pl.BlockSpec((None, tm, w), lambda i, j: (i, j, 0)) for w in widths]
        shapes = [jax.ShapeDtypeStruct((b, s, w), dt) for w, dt in zip(widths, dts)]
        extra_specs, extra = [_const_spec(wk.shape), _const_spec(wv.shape)], [wk, wv]
        sem = ("arbitrary", "arbitrary")
    return pl.pallas_call(
        functools.partial(_odd_proj_body, per_row=per_row),
        out_shape=shapes,
        grid=grid,
        in_specs=[xs, ms] + consts + [ts] * 6 + [_const_spec((256, 256))] + extra_specs,
        out_specs=outs,
        compiler_params=_cp(sem),
        name="odd_proj",
    )(x, mod, g, win, gcq, wuq, gq, gckv, gkpe, *qtabs, *ktabs, seg, *extra)


MLA_HG = 4


def _mla_prompt_body(q_ref, k_ref, v_ref, o_ref):
    qi = pl.program_id(2)
    lane = _lane_iota((TQ, LANES))
    chains = []
    for h in range(MLA_HG):
        chains.append((q_ref[:, h * LANES:(h + 1) * LANES],
                       lambda j, h=h: k_ref[pl.ds(pl.multiple_of(j * TQ, TQ), TQ), h * LANES:(h + 1) * LANES],
                       lambda j, h=h: v_ref[pl.ds(pl.multiple_of(j * TQ, TQ), TQ), (h // 2) * LANES:(h // 2 + 1) * LANES]))
    outs = _flash(qi, chains, post_scale=MLA_SCALE)
    for g2 in range(MLA_HG // 2):
        pair = jnp.where(lane < MLA_V, outs[2 * g2], outs[2 * g2 + 1])
        o_ref[:, g2 * LANES:(g2 + 1) * LANES] = pair.astype(BF16)


def _mla_prompt(q, kf, vf):
    b, s, _ = q.shape
    ng = MLA_HEADS // MLA_HG
    return pl.pallas_call(
        _mla_prompt_body,
        out_shape=jax.ShapeDtypeStruct((b, s, MLA_HEADS * MLA_V), BF16),
        grid=(b, ng, s // TQ),
        in_specs=[pl.BlockSpec((None, TQ, MLA_HG * LANES), lambda i, g, j: (i, j, g)),
                  pl.BlockSpec((None, s, MLA_HG * LANES), lambda i, g, j: (i, 0, g)),
                  pl.BlockSpec((None, s, MLA_HG * MLA_V), lambda i, g, j: (i, 0, g))],
        out_specs=pl.BlockSpec((None, TQ, MLA_HG * MLA_V), lambda i, g, j: (i, j, g)),
        compiler_params=_cp(("arbitrary", "arbitrary", "arbitrary")),
        name="mla_prompt",
    )(q, kf, vf)


def _chunk_copies(pt_ref, seq, chunk, cache_ref, buf_ref, sem_ref, slot, pages, page_base):
    cps = []
    for pi in range(pages):
        page = pt_ref[seq, chunk * pages + pi] + page_base
        cps.append(pltpu.make_async_copy(cache_ref.at[page], buf_ref.at[slot, pl.ds(pi * PAGE, PAGE)],
                                         sem_ref.at[slot]))
    return cps


def _start(cps):
    for c in cps:
        c.start()


def _wait(cps):
    for c in cps:
        c.wait()


def _decode_softmax_step(s, m, l, acc, v):
    m_new = jnp.maximum(m, jnp.max(s, axis=1, keepdims=True))
    alpha = jnp.exp(m - m_new)
    p = jnp.exp(s - m_new)
    return m_new, alpha * l + jnp.sum(p, axis=1, keepdims=True), alpha * acc + _dot(p.astype(BF16), v)


def _new_keep(rows, t_new, t_of_row):
    c = lax.broadcasted_iota(jnp.int32, (rows, 8), 1)
    return (c <= t_of_row) & (c < t_new)


def _pad8(x):
    return jnp.concatenate([x, jnp.zeros((8 - x.shape[0], x.shape[1]), x.dtype)], axis=0)


def _moba_sample_body(pt_ref, wq_ref, kn_ref, vn_ref, kc_ref, vc_ref, o_ref,
                      buf_ref, sem_ref, s_ref, p_ref, km_ref, *, pages, nch, page_base, t_new):
    b = pl.program_id(0)
    nb = pl.num_programs(0)
    ck = pages * PAGE
    bpc = ck // MOBA_BLOCK
    rows = wq_ref.shape[0]

    def copies(seq, u, slot):
        if isinstance(u, int):
            if u < nch:
                return _chunk_copies(pt_ref, seq, u, kc_ref, buf_ref, sem_ref, slot, pages, page_base)
            return _chunk_copies(pt_ref, seq, u - nch, vc_ref, buf_ref, sem_ref, slot, pages, page_base)
        raise TypeError

    @pl.when(b == 0)
    def _():
        _start(copies(b, 0, 0))

    wq = wq_ref[...]
    wqb = (wq * (MOBA_D ** -0.5)).astype(BF16)

    for u in range(2 * nch):
        slot = u % 2
        if u + 1 < 2 * nch:
            _start(copies(b, u + 1, 1 - slot))
        else:
            @pl.when(b + 1 < nb)
            def _():
                _start(copies(b + 1, 0, 1 - slot))
        _wait(copies(b, u, slot))
        if u < nch:
            kc = buf_ref[slot]
            km_ref[u * bpc:(u + 1) * bpc, :] = jnp.mean(kc.reshape(bpc, MOBA_BLOCK, LANES), axis=1)
            s_ref[u] = _dot_nt(wqb, kc.astype(BF16))
        if u == nch - 1:
            nblk = nch * bpc
            gate = _dot_nt(wq, km_ref[...], precision=lax.Precision.HIGHEST)
            lane = _lane_iota((rows, nblk))
            g = gate
            picked = jnp.zeros((rows, nblk), jnp.bool_)
            for _ in range(min(MOBA_TOPK, nblk)):
                mx = jnp.max(g, axis=1, keepdims=True)
                first = jnp.min(jnp.where(g == mx, lane, nblk), axis=1, keepdims=True)
                hit = lane == first
                picked = picked | hit
                g = jnp.where(hit, BELOW_MASK, g)
            bias = jnp.where(picked, 0.0, MASK)
            t_row = lax.broadcasted_iota(jnp.int32, (rows, 8), 0) % t_new
            s_new = jnp.where(_new_keep(rows, t_new, t_row),
                              _dot_nt(wqb, _pad8(kn_ref[...]).astype(BF16)), MASK)
            m = jnp.max(s_new, axis=1, keepdims=True)
            for c in range(nch):
                for j in range(bpc):
                    sb = s_ref[c, :, j * MOBA_BLOCK:(j + 1) * MOBA_BLOCK] + bias[:, c * bpc + j:c * bpc + j + 1]
                    m = jnp.maximum(m, jnp.max(sb, axis=1, keepdims=True))
            p_new = jnp.exp(s_new - m)
            l = jnp.sum(p_new, axis=1, keepdims=True)
            for c in range(nch):
                for j in range(bpc):
                    sb = s_ref[c, :, j * MOBA_BLOCK:(j + 1) * MOBA_BLOCK] + bias[:, c * bpc + j:c * bpc + j + 1]
                    p = jnp.exp(sb - m)
                    l = l + jnp.sum(p, axis=1, keepdims=True)
                    p_ref[c, :, j * MOBA_BLOCK:(j + 1) * MOBA_BLOCK] = p.astype(BF16)
            acc = _dot(p_new.astype(BF16), _pad8(vn_ref[...]).astype(BF16))
        if u >= nch:
            acc = acc + _dot(p_ref[u - nch], buf_ref[slot].astype(BF16))
    o_ref[...] = acc / l


def _moba_sample(pt, wq, kn, vn, kcache, vcache, layer, pages):
    nb, rows, _ = wq.shape
    t_new = kn.shape[1]
    n_pages = pt.shape[1]
    nch = n_pages // pages
    n_phys = kcache.shape[1]
    kc = kcache.reshape(-1, PAGE, LANES)
    vc = vcache.reshape(-1, PAGE, LANES)
    ck = pages * PAGE
    grid_spec = pltpu.PrefetchScalarGridSpec(
        num_scalar_prefetch=1, grid=(nb,),
        in_specs=[pl.BlockSpec((None, rows, LANES), lambda i, p: (i, 0, 0)),
                  pl.BlockSpec((None, t_new, LANES), lambda i, p: (i, 0, 0)),
                  pl.BlockSpec((None, t_new, LANES), lambda i, p: (i, 0, 0)),
                  pl.BlockSpec(memory_space=pl.ANY), pl.BlockSpec(memory_space=pl.ANY)],
        out_specs=pl.BlockSpec((None, rows, LANES), lambda i, p: (i, 0, 0)),
        scratch_shapes=[pltpu.VMEM((2, ck, LANES), F32), pltpu.SemaphoreType.DMA((2,)),
                        pltpu.VMEM((nch, rows, ck), F32), pltpu.VMEM((nch, rows, ck), BF16),
                        pltpu.VMEM((n_pages * PAGE // MOBA_BLOCK, LANES), F32)])
    return pl.pallas_call(
        functools.partial(_moba_sample_body, pages=pages, nch=nch, page_base=layer * n_phys, t_new=t_new),
        out_shape=jax.ShapeDtypeStruct((nb, rows, LANES), F32),
        grid_spec=grid_spec,
        compiler_params=_cp(("arbitrary",)),
        name="moba_sample",
    )(pt, wq, kn, vn, kc, vc)


def _diff_sample_body(pt_ref, wq_ref, kn_ref, vn_ref, lv_ref, gs_ref, kc_ref, vc_ref, o_ref,
                      kbuf_ref, vbuf_ref, ksem_ref, vsem_ref, *, pages, nch, page_base, t_new, lam_init):
    b = pl.program_id(0)
    nb = pl.num_programs(0)
    rows = wq_ref.shape[0]

    def copies(seq, c, slot):
        return (_chunk_copies(pt_ref, seq, c, kc_ref, kbuf_ref, ksem_ref, slot, pages, page_base)
                + _chunk_copies(pt_ref, seq, c, vc_ref, vbuf_ref, vsem_ref, slot, pages, page_base))

    @pl.when(b == 0)
    def _():
        _start(copies(b, 0, 0))

    wqb = (wq_ref[...] * (DIFF_D ** -0.5)).astype(BF16)
    t_row = lax.broadcasted_iota(jnp.int32, (rows, 8), 0) % t_new
    s_new = jnp.where(_new_keep(rows, t_new, t_row), _dot_nt(wqb, _pad8(kn_ref[...]).astype(BF16)), MASK)
    m = jnp.max(s_new, axis=1, keepdims=True)
    p_new = jnp.exp(s_new - m)
    l = jnp.sum(p_new, axis=1, keepdims=True)
    acc = _dot(p_new.astype(BF16), _pad8(vn_ref[...]).astype(BF16))

    for c in range(nch):
        slot = c % 2
        if c + 1 < nch:
            _start(copies(b, c + 1, 1 - slot))
        else:
            @pl.when(b + 1 < nb)
            def _():
                _start(copies(b + 1, 0, 1 - slot))
        _wait(copies(b, c, slot))
        s = _dot_nt(wqb, kbuf_ref[slot].astype(BF16))
        m, l, acc = _decode_softmax_step(s, m, l, acc, vbuf_ref[slot].astype(BF16))

    accn = acc / l
    lam = _lambda(lv_ref, lam_init)
    grp = DIFF_HEADS // DIFF_KV
    gt = grp * t_new
    for k in range(DIFF_KV):
        a0 = accn[k * 2 * gt:k * 2 * gt + gt, k * LANES:(k + 1) * LANES]
        a1 = accn[k * 2 * gt + gt:(k + 1) * 2 * gt, k * LANES:(k + 1) * LANES]
        o_ref[k * gt:(k + 1) * gt, :] = _rms(a0 - lam * a1, gs_ref[...]) * (1.0 - lam_init)


def _diff_sample(pt, wq, kn, vn, lam_v, g_subln, kcache, vcache, layer, pages, lam_init):
    nb, rows, w = wq.shape
    t_new = kn.shape[1]
    n_pages = pt.shape[1]
    nch = n_pages // pages
    n_phys = kcache.shape[1]
    kc = kcache.reshape(-1, PAGE, w)
    vc = vcache.reshape(-1, PAGE, w)
    ck = pages * PAGE
    grid_spec = pltpu.PrefetchScalarGridSpec(
        num_scalar_prefetch=1, grid=(nb,),
        in_specs=[pl.BlockSpec((None, rows, w), lambda i, p: (i, 0, 0)),
                  pl.BlockSpec((None, t_new, w), lambda i, p: (i, 0, 0)),
                  pl.BlockSpec((None, t_new, w), lambda i, p: (i, 0, 0)),
                  pl.BlockSpec((4, DIFF_D), lambda i, p: (0, 0)),
                  pl.BlockSpec((1, 2 * DIFF_D), lambda i, p: (0, 0)),
                  pl.BlockSpec(memory_space=pl.ANY), pl.BlockSpec(memory_space=pl.ANY)],
        out_specs=pl.BlockSpec((None, rows // 2, LANES), lambda i, p: (i, 0, 0)),
        scratch_shapes=[pltpu.VMEM((2, ck, w), F32), pltpu.VMEM((2, ck, w), F32),
                        pltpu.SemaphoreType.DMA((2,)), pltpu.SemaphoreType.DMA((2,))])
    return pl.pallas_call(
        functools.partial(_diff_sample_body, pages=pages, nch=nch, page_base=layer * n_phys, t_new=t_new,
                          lam_init=lam_init),
        out_shape=jax.ShapeDtypeStruct((nb, rows // 2, LANES), F32),
        grid_spec=grid_spec,
        compiler_params=_cp(("arbitrary",)),
        name="diff_sample",
    )(pt, wq, kn, vn, lam_v, g_subln.reshape(1, -1), kc, vc)


def _mla_sample_body(pt_ref, ql_ref, qp_ref, cn_ref, pn_ref, cc_ref, pc_ref, o_ref,
                     cbuf_ref, pbuf_ref, csem_ref, psem_ref, *, pages, nch, page_base, t_new):
    b = pl.program_id(0)
    nb = pl.num_programs(0)
    rows = ql_ref.shape[0]

    def copies(seq, c, slot):
        return (_chunk_copies(pt_ref, seq, c, cc_ref, cbuf_ref, csem_ref, slot, pages, page_base)
                + _chunk_copies(pt_ref, seq, c, pc_ref, pbuf_ref, psem_ref, slot, pages, page_base))

    @pl.when(b == 0)
    def _():
        _start(copies(b, 0, 0))

    ql, qp = ql_ref[...], qp_ref[...]
    cn = _pad8(cn_ref[...]).astype(BF16)
    t_row = lax.broadcasted_iota(jnp.int32, (rows, 8), 0) // (rows // t_new)
    s_new = (_dot_nt(ql, cn) + _dot_nt(qp, _pad8(pn_ref[...]).astype(BF16))) * MLA_SCALE
    s_new = jnp.where(_new_keep(rows, t_new, t_row), s_new, MASK)
    m = jnp.max(s_new, axis=1, keepdims=True)
    p_new = jnp.exp(s_new - m)
    l = jnp.sum(p_new, axis=1, keepdims=True)
    acc = _dot(p_new.astype(BF16), cn)

    for c in range(nch):
        slot = c % 2
        if c + 1 < nch:
            _start(copies(b, c + 1, 1 - slot))
        else:
            @pl.when(b + 1 < nb)
            def _():
                _start(copies(b + 1, 0, 1 - slot))
        _wait(copies(b, c, slot))
        cb = cbuf_ref[slot].astype(BF16)
        s = (_dot_nt(ql, cb) + _dot_nt(qp, pbuf_ref[slot].astype(BF16))) * MLA_SCALE
        m, l, acc = _decode_softmax_step(s, m, l, acc, cb)
    o_ref[...] = (acc / l).astype(BF16)


def _mla_sample(pt, ql, qp, cn, pn, ccache, pcache, layer, pages):
    nb, rows, _ = ql.shape
    t_new = cn.shape[1]
    n_pages = pt.shape[1]
    nch = n_pages // pages
    n_phys = ccache.shape[1]
    cc = ccache.reshape(-1, PAGE, MLA_KVL)
    pc = pcache.reshape(-1, PAGE, MLA_ROPE)
    ck = pages * PAGE
    grid_spec = pltpu.PrefetchScalarGridSpec(
        num_scalar_prefetch=1, grid=(nb,),
        in_specs=[pl.BlockSpec((None, rows, MLA_KVL), lambda i, p: (i, 0, 0)),
                  pl.BlockSpec((None, rows, MLA_ROPE), lambda i, p: (i, 0, 0)),
                  pl.BlockSpec((None, t_new, MLA_KVL), lambda i, p: (i, 0, 0)),
                  pl.BlockSpec((None, t_new, MLA_ROPE), lambda i, p: (i, 0, 0)),
                  pl.BlockSpec(memory_space=pl.ANY), pl.BlockSpec(memory_space=pl.ANY)],
        out_specs=pl.BlockSpec((None, rows, MLA_KVL), lambda i, p: (i, 0, 0)),
        scratch_shapes=[pltpu.VMEM((2, ck, MLA_KVL), F32), pltpu.VMEM((2, ck, MLA_ROPE), F32),
                        pltpu.SemaphoreType.DMA((2,)), pltpu.SemaphoreType.DMA((2,))])
    return pl.pallas_call(
        functools.partial(_mla_sample_body, pages=pages, nch=nch, page_base=layer * n_phys, t_new=t_new),
        out_shape=jax.ShapeDtypeStruct((nb, rows, MLA_KVL), BF16),
        grid_spec=grid_spec,
        compiler_params=_cp(("arbitrary",)),
        name="mla_sample",
    )(pt, ql, qp, cn, pn, cc, pc)


def _mla_qlat_body(q_ref, wuk_ref, ql_ref, qp_ref):
    for h in range(MLA_HEADS):
        qh = q_ref[:, h * LANES:(h + 1) * LANES]
        ql_ref[:, h * MLA_KVL:(h + 1) * MLA_KVL] = _dot(qh, wuk_ref[h]).astype(BF16)
        qp_ref[:, h * MLA_ROPE:(h + 1) * MLA_ROPE] = qh[:, MLA_NOPE:MLA_NOPE + MLA_ROPE]


def _mla_qlat(q, wuk_pad):
    nt = q.shape[0]
    return pl.pallas_call(
        _mla_qlat_body,
        out_shape=[jax.ShapeDtypeStruct((nt, MLA_HEADS * MLA_KVL), BF16),
                   jax.ShapeDtypeStruct((nt, MLA_HEADS * MLA_ROPE), BF16)],
        name="mla_qlat",
        compiler_params=pltpu.CompilerParams(vmem_limit_bytes=VMEM_LIMIT),
    )(q, wuk_pad)


def _mla_oup_body(ol_ref, wuv_ref, o_ref):
    for h in range(MLA_HEADS):
        o_ref[:, h * MLA_V:(h + 1) * MLA_V] = _dot(ol_ref[:, h * MLA_KVL:(h + 1) * MLA_KVL], wuv_ref[h]).astype(BF16)


def _mla_oup(ol, wuv):
    nt = ol.shape[0]
    return pl.pallas_call(
        _mla_oup_body,
        out_shape=jax.ShapeDtypeStruct((nt, MLA_HEADS * MLA_V), BF16),
        name="mla_oup",
        compiler_params=pltpu.CompilerParams(vmem_limit_bytes=VMEM_LIMIT),
    )(ol, wuv)


def _rope_tables(pos, theta, rot, period, offset):
    half = rot // 2
    inv_freq = theta ** (-jnp.arange(half, dtype=F32) / half)
    ang = pos.astype(F32)[:, None] * inv_freq[None, :]
    cos, sin = jnp.cos(ang), jnp.sin(ang)
    n = pos.shape[0]
    e = np.arange(LANES) % period - offset
    first = (e >= 0) & (e < half)
    second = (e >= half) & (e < rot)
    idx = np.where(first, e, np.where(second, e - half, 0))
    cg, sg = cos[:, idx], sin[:, idx]
    c = jnp.where(first | second, cg, jnp.ones((n, LANES), F32))
    s1 = jnp.where(first, -sg, 0.0)
    s2 = jnp.where(second, sg, 0.0)
    return c, s1, s2


def _seg_matrix(seg):
    i = np.arange(256)
    return jnp.asarray((i[:, None] // seg) == (i[None, :] // seg), BF16)


def kernel(x_prompt, x_sample, cache_moba_k, cache_moba_v, cache_diff_k, cache_diff_v, cache_mla_ckv, cache_mla_kpe, page_table, c_prompt, c_sample, ada_w, ada_b, norm_g, ffn1_w_gu, ffn1_w_down, ffn2_w_gu, ffn2_w_down, ev_w_in, ev_g_q_moba, ev_g_k_moba, ev_g_q_diff, ev_g_k_diff, ev_lambda, ev_g_subln, ev_w_out, od_w_in, od_g_cq, od_w_uq, od_g_q, od_g_ckv, od_g_kpe, od_w_ukv, od_w_out):
    b, s, d = x_prompt.shape
    nb, t_new, _ = x_sample.shape
    nt = nb * t_new
    depth = ada_w.shape[0]
    n_pages = page_table.shape[1]
    past = n_pages * PAGE
    assert d == D_MODEL and s % 512 == 0 and past % MOBA_BLOCK == 0 and t_new <= 8 and nt % 8 == 0
    tms = 256 if nt % 256 == 0 else nt
    pages = 16 if n_pages % 32 == 0 else n_pages // 2
    assert n_pages % pages == 0 and (n_pages // pages) % 2 == 0

    pos_p = jnp.arange(s, dtype=jnp.int32)
    pos_s = jnp.tile(past + jnp.arange(t_new, dtype=jnp.int32), nb)
    tabs_even = (_rope_tables(pos_p, ROPE_THETA, MOBA_D // 4, MOBA_D, 0),
                 _rope_tables(pos_s, ROPE_THETA, MOBA_D // 4, MOBA_D, 0))
    tabs_q = (_rope_tables(pos_p, MLA_ROPE_THETA, MLA_ROPE, LANES, MLA_NOPE),
              _rope_tables(pos_s, MLA_ROPE_THETA, MLA_ROPE, LANES, MLA_NOPE))
    tabs_k = (_rope_tables(pos_p, MLA_ROPE_THETA, MLA_ROPE, LANES, 0),
              _rope_tables(pos_s, MLA_ROPE_THETA, MLA_ROPE, LANES, 0))
    seg64, seg128 = _seg_matrix(MOBA_D), _seg_matrix(LANES)

    mod = _adaln(jnp.concatenate([c_prompt, c_sample], axis=0), ada_w, ada_b)
    xp, xs = x_prompt, x_sample.reshape(nt, d)
    rows_p = [[] for _ in range(6)]
    rows_s = [[] for _ in range(6)]
    for l in range(depth):
        mod_p = mod[l, :b].reshape(b, N_MOD, d)
        mod_s = jnp.repeat(mod[l, b:].reshape(nb, N_MOD, d).transpose(1, 0, 2), t_new, axis=1)
        g = norm_g[l]
        w1 = (ffn1_w_gu[l].astype(BF16), ffn1_w_down[l].astype(BF16))
        w2 = (ffn2_w_gu[l].astype(BF16), ffn2_w_down[l].astype(BF16))
        xp = _ffn_prompt(xp, mod_p, g[0:1], *w1, 0)
        xs = _ffn_sample(xs, mod_s, g[0:1], *w1, 0, tms)
        if l % 2 == 0:
            e = l // 2
            lam_init = 0.8 - 0.6 * math.exp(-0.3 * l)
            gain = jnp.concatenate([jnp.tile(ev_g_q_moba[e], MOBA_HEADS), jnp.tile(ev_g_k_moba[e], MOBA_KV),
                                    jnp.ones((MOBA_KV * MOBA_D,), F32), jnp.tile(ev_g_q_diff[e], 2 * DIFF_HEADS),
                                    jnp.tile(ev_g_k_diff[e], 2 * DIFF_KV),
                                    jnp.ones((2 * DIFF_KV * DIFF_D,), F32)]).reshape(1, EVEN_IN)
            win = ev_w_in[e].astype(BF16)
            wout = ev_w_out[e].astype(BF16)
            qm, km, vm, qd, kd, vd = _even_proj(xp, mod_p, g[1:2], win, gain, tabs_even[0], seg64, False, 512)
            om = _moba_prompt(qm, km, vm)
            od = _diff_prompt(qd, kd, vd, ev_lambda[e], ev_g_subln[e], lam_init)
            xp = _out_proj(xp, mod_p, [om, od], wout, False, 512)
            new_p = (km.reshape(b, s, MOBA_KV, MOBA_D), vm.reshape(b, s, MOBA_KV, MOBA_D),
                     kd.reshape(b, s, DIFF_KV, 2 * DIFF_D), vd.reshape(b, s, DIFF_KV, 2 * DIFF_D))
            qm, km, vm, qd, kd, vd = _even_proj(xs, mod_s, g[1:2], win, gain, tabs_even[1], seg64, True, tms)
            hpk = MOBA_HEADS // MOBA_KV
            q4 = qm.reshape(nb, t_new, MOBA_KV, hpk, MOBA_D).transpose(0, 2, 3, 1, 4)
            z = jnp.zeros_like(q4)
            wq = jnp.stack([jnp.concatenate([q4[:, 0], z[:, 0]], axis=-1),
                            jnp.concatenate([z[:, 1], q4[:, 1]], axis=-1)], axis=1).reshape(nb, MOBA_HEADS * t_new, LANES)
            om = _moba_sample(page_table, wq, km.reshape(nb, t_new, LANES), vm.reshape(nb, t_new, LANES),
                              cache_moba_k, cache_moba_v, e, pages)
            om = om.reshape(nb, MOBA_KV, hpk, t_new, MOBA_KV, MOBA_D)
            om = jnp.stack([om[:, 0, :, :, 0], om[:, 1, :, :, 1]], axis=1)
            om = om.transpose(0, 3, 1, 2, 4).reshape(nt, MOBA_HEADS * MOBA_D)
            grp = DIFF_HEADS // DIFF_KV
            q5 = qd.reshape(nb, t_new, DIFF_KV, grp, 2, DIFF_D).transpose(0, 2, 4, 3, 1, 5)
            z = jnp.zeros_like(q5[:, 0, 0])
            blocks = []
            for k in range(DIFF_KV):
                for m_ in range(2):
                    parts = [z] * (2 * DIFF_KV)
                    parts[2 * k + m_] = q5[:, k, m_]
                    blocks.append(jnp.concatenate(parts, axis=-1))
            wqd = jnp.stack(blocks, axis=1).reshape(nb, 2 * DIFF_HEADS * t_new, 2 * DIFF_KV * DIFF_D)
            od = _diff_sample(page_table, wqd, kd.reshape(nb, t_new, -1), vd.reshape(nb, t_new, -1),
                              ev_lambda[e], ev_g_subln[e], cache_diff_k, cache_diff_v, e, pages, lam_init)
            od = od.reshape(nb, DIFF_HEADS, t_new, 2 * DIFF_D).transpose(0, 2, 1, 3).reshape(nt, DIFF_HEADS * 2 * DIFF_D)
            xs = _out_proj(xs, mod_s, [om, od], wout, True, tms)
            new_s = (km.reshape(nb, t_new, MOBA_KV, MOBA_D), vm.reshape(nb, t_new, MOBA_KV, MOBA_D),
                     kd.reshape(nb, t_new, DIFF_KV, 2 * DIFF_D), vd.reshape(nb, t_new, DIFF_KV, 2 * DIFF_D))
            off = 0
        else:
            o = l // 2
            hd = MLA_NOPE + MLA_ROPE
            win = jnp.pad(od_w_in[o], ((0, 0), (0, ODD_IN_PAD - od_w_in.shape[-1]))).astype(BF16)
            wuq = jnp.pad(od_w_uq[o].reshape(MLA_QL, MLA_HEADS, hd), ((0, 0), (0, 0), (0, LANES - hd)))
            wuq = wuq.reshape(MLA_QL, MLA_HEADS * LANES).astype(BF16)
            gq = jnp.tile(jnp.pad(od_g_q[o], (0, LANES - hd)), 2).reshape(1, 256)
            gkpe = jnp.pad(od_g_kpe[o], (0, LANES - MLA_ROPE)).reshape(1, LANES)
            wukv = od_w_ukv[o]
            wuk, wuv = wukv[..., :MLA_NOPE], wukv[..., MLA_NOPE:]
            wk_top = jnp.pad(wuk, ((0, 0), (0, 0), (0, LANES - MLA_NOPE))).reshape(MLA_KVL, MLA_HEADS * LANES)
            eye = jnp.zeros((LANES, LANES), F32).at[jnp.arange(MLA_ROPE), MLA_NOPE + jnp.arange(MLA_ROPE)].set(1.0)
            wk = jnp.concatenate([wk_top, jnp.tile(eye, (1, MLA_HEADS))], axis=0).astype(BF16)
            wv = wuv.reshape(MLA_KVL, MLA_HEADS * MLA_V).astype(BF16)
            wout = od_w_out[o].astype(BF16)
            args = (win, od_g_cq[o].reshape(1, -1), wuq, gq, od_g_ckv[o].reshape(1, -1), gkpe)
            q, ckv, kpe, kf, vf = _odd_proj(xp, mod_p, g[1:2], *args, tabs_q[0], tabs_k[0], seg128, wk, wv, False, 512)
            op = _mla_prompt(q, kf, vf)
            xp = _out_proj(xp, mod_p, [op], wout, False, 512)
            new_p = (ckv, kpe)
            q, ckv, kpe = _odd_proj(xs, mod_s, g[1:2], *args, tabs_q[1], tabs_k[1], seg128, None, None, True, tms)
            wuk_pad = jnp.pad(wuk.transpose(1, 2, 0), ((0, 0), (0, LANES - MLA_NOPE), (0, 0))).astype(BF16)
            ql, qp = _mla_qlat(q, wuk_pad)
            ol = _mla_sample(page_table, ql.reshape(nb, t_new * MLA_HEADS, MLA_KVL),
                             qp.reshape(nb, t_new * MLA_HEADS, MLA_ROPE), ckv.reshape(nb, t_new, MLA_KVL),
                             kpe.reshape(nb, t_new, MLA_ROPE), cache_mla_ckv, cache_mla_kpe, o, pages)
            os_ = _mla_oup(ol.reshape(nt, MLA_HEADS * MLA_KVL), wuv.transpose(1, 0, 2).astype(BF16))
            xs = _out_proj(xs, mod_s, [os_], wout, True, tms)
            new_s = (ckv.reshape(nb, t_new, MLA_KVL), kpe.reshape(nb, t_new, MLA_ROPE))
            off = 4
        xp = _ffn_prompt(xp, mod_p, g[2:3], *w2, 2)
        xs = _ffn_sample(xs, mod_s, g[2:3], *w2, 2, tms)
        for i, (rp, rs) in enumerate(zip(new_p, new_s)):
            rows_p[off + i].append(rp)
            rows_s[off + i].append(rs)
    outs_p = [jnp.stack(r, axis=0) for r in rows_p]
    outs_s = [jnp.stack(r, axis=0) for r in rows_s]
    return (xp, xs.reshape(nb, t_new, d), *outs_p, *outs_s)
```
